```python
import jax, jax.numpy as jnp
from jax import lax
import numpy as np

D_MODEL = 1024
BATCH = 32
SEQ = 2048
DEPTH = 4

N_FOURIER_GROUPS = 8
D_FOURIER = D_MODEL // 2
FOURIER_GROUP_DIM = D_FOURIER // N_FOURIER_GROUPS
CHUNK = 128
N_SGU_HEADS = 8
D_SGU = D_MODEL
SGU_HEAD_DIM = D_SGU // N_SGU_HEADS
IN_COLS = D_FOURIER + 2 * D_SGU + 2 * D_MODEL
D_FF = 4 * D_MODEL
EPS = 1e-6

kernel_name = "hybrid_fnet_gmlp_gated_encoder"


def rms_norm(x, g):
    xf = x.astype(jnp.float32)
    y = xf * lax.rsqrt(jnp.mean(xf * xf, axis=-1, keepdims=True) + EPS)
    return (y * g.astype(jnp.float32)).astype(x.dtype)


def layer_norm(x, g, b):
    xf = x.astype(jnp.float32)
    mu = jnp.mean(xf, axis=-1, keepdims=True)
    xc = xf - mu
    y = xc * lax.rsqrt(jnp.mean(xc * xc, axis=-1, keepdims=True) + EPS)
    return (y * g.astype(jnp.float32) + b.astype(jnp.float32)).astype(x.dtype)


def fourier_mixer(a):
    bsz, seq, _ = a.shape
    ag = a.reshape(bsz, seq, N_FOURIER_GROUPS, FOURIER_GROUP_DIM).astype(jnp.float32)
    y = jnp.fft.fft2(ag, axes=(1, 3), norm="ortho").real
    return y.reshape(bsz, seq, D_FOURIER).astype(a.dtype)


def spatial_gating(u, v, ln_g, ln_b, w_s, b_s):
    bsz, seq, _ = v.shape
    n_chunks = seq // CHUNK
    vn = layer_norm(v, ln_g, ln_b).reshape(bsz, n_chunks, CHUNK, N_SGU_HEADS, SGU_HEAD_DIM)
    mixed = jnp.einsum('hqp,bnphd->bnqhd', w_s, vn) + b_s.T[None, None, :, :, None]
    return u * mixed.reshape(bsz, seq, D_SGU)


def setup_inputs(seed: int = 0) -> dict:
    key = jax.random.key(seed)
    ks = jax.random.split(key, 16)
    f32 = jnp.float32
    nrm = lambda k, shape, scale: jax.random.normal(k, shape, f32) * scale
    return {
        "x": nrm(ks[0], (BATCH, SEQ, D_MODEL), 1.0),
        "g_mix": 1.0 + nrm(ks[1], (DEPTH, D_MODEL), 0.02),
        "w_in": nrm(ks[2], (DEPTH, D_MODEL, IN_COLS), D_MODEL ** -0.5),
        "w_a": nrm(ks[3], (DEPTH, D_FOURIER, D_MODEL), D_FOURIER ** -0.5),
        "ln_v_g": 1.0 + nrm(ks[4], (DEPTH, D_SGU), 0.02),
        "ln_v_b": nrm(ks[5], (DEPTH, D_SGU), 0.02),
        "w_s": nrm(ks[6], (DEPTH, N_SGU_HEADS, CHUNK, CHUNK), CHUNK ** -0.5),
        "b_s": 1.0 + nrm(ks[7], (DEPTH, N_SGU_HEADS, CHUNK), 0.1),
        "w_b": nrm(ks[8], (DEPTH, D_SGU, D_MODEL), D_SGU ** -0.5),
        "w_out": nrm(ks[9], (DEPTH, D_MODEL, D_MODEL), D_MODEL ** -0.5),
        "g_mlp": 1.0 + nrm(ks[10], (DEPTH, D_MODEL), 0.02),
        "w_up": nrm(ks[11], (DEPTH, D_MODEL, D_FF), D_MODEL ** -0.5),
        "w_down": nrm(ks[12], (DEPTH, D_FF, D_MODEL), D_FF ** -0.5),
        "g_final": 1.0 + nrm(ks[13], (D_MODEL,), 0.02),
    }


def reference(x, g_mix, w_in, w_a, ln_v_g, ln_v_b, w_s, b_s, w_b, w_out,
              g_mlp, w_up, w_down, g_final):
    c_a = D_FOURIER
    c_u = c_a + D_SGU
    c_v = c_u + D_SGU
    c_ga = c_v + D_MODEL
    for l in range(DEPTH):
        h = rms_norm(x, g_mix[l])
        z = jnp.einsum('bsd,dc->bsc', h, w_in[l])
        a_in = z[..., :c_a]
        uv = jax.nn.gelu(z[..., c_a:c_v])
        u, v = uv[..., :D_SGU], uv[..., D_SGU:]
        gate_a = jax.nn.sigmoid(z[..., c_v:c_ga])
        gate_b = jax.nn.sigmoid(z[..., c_ga:])
        y_a = jnp.einsum('bsc,cd->bsd', fourier_mixer(a_in), w_a[l])
        y_b = jnp.einsum('bsc,cd->bsd', spatial_gating(u, v, ln_v_g[l], ln_v_b[l], w_s[l], b_s[l]), w_b[l])
        merged = gate_a * y_a + gate_b * y_b
        x = x + jnp.einsum('bsd,de->bse', merged, w_out[l])
        h2 = rms_norm(x, g_mlp[l])
        f = jnp.square(jax.nn.relu(jnp.einsum('bsd,df->bsf', h2, w_up[l])))
        x = x + jnp.einsum('bsf,fd->bsd', f, w_down[l])
    return rms_norm(x, g_final)
```

```python
import functools
import math

import jax
import jax.numpy as jnp
from jax import lax
from jax.experimental import pallas as pl
from jax.experimental.pallas import tpu as pltpu

D_MODEL = 1024
N_FOURIER_GROUPS = 8
D_FOURIER = D_MODEL // 2
FOURIER_GROUP_DIM = D_FOURIER // N_FOURIER_GROUPS
CHUNK = 128
N_SGU_HEADS = 8
D_SGU = D_MODEL
SGU_HEAD_DIM = D_SGU // N_SGU_HEADS
D_FF = 4 * D_MODEL
EPS = 1e-6

C_A = D_FOURIER
C_U = C_A + D_SGU
C_V = C_U + D_SGU
C_GA = C_V + D_MODEL
IN_COLS = C_GA + D_MODEL

V7X_VMEM_LIMIT_BYTES = 56 * 1024 * 1024

TOKENS_PER_STEP = 512
FF_COLS_PER_DOT = 1024

BF16 = jnp.bfloat16
F32 = jnp.float32


def _dot(a, b):
    return jnp.dot(a, b, preferred_element_type=F32)


def _rms_norm(x, g):
    return x * lax.rsqrt(jnp.mean(x * x, axis=-1, keepdims=True) + EPS) * g


def _gelu_tanh(x):
    c = math.sqrt(2.0 / math.pi)
    return 0.5 * x * (1.0 + jnp.tanh(c * (x + 0.044715 * (x * x * x))))


def _sigmoid(x):
    return 1.0 / (1.0 + jnp.exp(-x))


def _resident(shape):
    return pl.BlockSpec(shape, lambda *_: (0,) * len(shape), pipeline_mode=pl.Buffered(1))


def _mixer_in_body(x_ref, g_ref, w_in_ref, cs_ref, ln_g_ref, ln_b_ref, w_s_ref, bias_ref, w_b_ref,
                   pq_ref, ga_ref, mb_ref):
    hb = _rms_norm(x_ref[...], g_ref[...]).astype(BF16)

    a = _dot(hb, w_in_ref[:, 0:C_A]).astype(BF16)
    pq_ref[...] = _dot(a, cs_ref[...]).astype(BF16)

    ga_ref[...] = _sigmoid(_dot(hb, w_in_ref[:, C_V:C_GA])).astype(BF16)

    u = _gelu_tanh(_dot(hb, w_in_ref[:, C_A:C_U]))
    v = _gelu_tanh(_dot(hb, w_in_ref[:, C_U:C_V]))
    mu = jnp.mean(v, axis=-1, keepdims=True)
    vc = v - mu
    vn = vc * lax.rsqrt(jnp.mean(vc * vc, axis=-1, keepdims=True) + EPS)
    vn = (vn * ln_g_ref[...] + ln_b_ref[...]).astype(BF16)

    n_chunks = x_ref.shape[0] // CHUNK
    rows = []
    for c in range(n_chunks):
        r0 = c * CHUNK
        cols = []
        for hd in range(N_SGU_HEADS):
            c0 = hd * SGU_HEAD_DIM
            cols.append(_dot(w_s_ref[hd], vn[r0:r0 + CHUNK, c0:c0 + SGU_HEAD_DIM]))
        rows.append(jnp.concatenate(cols, axis=1) + bias_ref[...])
    mixed = jnp.concatenate(rows, axis=0)
    s = (u * mixed).astype(BF16)
    yb = _dot(s, w_b_ref[...])
    gb = _sigmoid(_dot(hb, w_in_ref[:, C_GA:IN_COLS]))
    mb_ref[...] = (gb * yb).astype(BF16)


def _mixer_in(x2d, g, w_in, cs, ln_g, ln_b, w_s, bias, w_b):
    n_tok = x2d.shape[0]
    t = TOKENS_PER_STEP
    tok = lambda width: pl.BlockSpec((t, width), lambda i: (i, 0))
    out = jax.ShapeDtypeStruct((n_tok, D_MODEL), BF16)
    return pl.pallas_call(
        _mixer_in_body,
        grid=(n_tok // t,),
        in_specs=[tok(D_MODEL), _resident((1, D_MODEL)), _resident((D_MODEL, IN_COLS)),
                  _resident((D_FOURIER, 2 * D_FOURIER)), _resident((1, D_SGU)), _resident((1, D_SGU)),
                  _resident((N_SGU_HEADS, CHUNK, CHUNK)), _resident((CHUNK, D_SGU)),
                  _resident((D_SGU, D_MODEL))],
        out_specs=[tok(2 * D_FOURIER), tok(D_MODEL), tok(D_MODEL)],
        out_shape=[out, out, out],
        compiler_params=pltpu.CompilerParams(dimension_semantics=("arbitrary",),
                                             vmem_limit_bytes=V7X_VMEM_LIMIT_BYTES),
        name="mixer_in",
    )(x2d, g, w_in, cs, ln_g, ln_b, w_s, bias, w_b)


def _mixer_out_body(dft_ref, pq_ref, x_ref, ga_ref, mb_ref, w_a_ref, w_out_ref, o_ref):
    seq = pq_ref.shape[1]
    y = _dot(dft_ref[:, 0:seq], pq_ref[0, :, 0:D_FOURIER])
    y = y + _dot(dft_ref[:, seq:2 * seq], pq_ref[0, :, D_FOURIER:2 * D_FOURIER])
    ya = _dot(y.astype(BF16), w_a_ref[...])
    merged = ga_ref[0].astype(F32) * ya + mb_ref[0].astype(F32)
    o_ref[0] = x_ref[0] + _dot(merged.astype(BF16), w_out_ref[...])


def _mixer_out(dft, pq, x, ga, mb, w_a, w_out):
    bsz, seq, _ = x.shape
    t = TOKENS_PER_STEP
    tok = lambda width: pl.BlockSpec((1, t, width), lambda b, i: (b, i, 0))
    return pl.pallas_call(
        _mixer_out_body,
        grid=(bsz, seq // t),
        in_specs=[pl.BlockSpec((t, 2 * seq), lambda b, i: (i, 0)),
                  pl.BlockSpec((1, seq, 2 * D_FOURIER), lambda b, i: (b, 0, 0)),
                  tok(D_MODEL), tok(D_MODEL), tok(D_MODEL),
                  _resident((D_FOURIER, D_MODEL)), _resident((D_MODEL, D_MODEL))],
        out_specs=tok(D_MODEL),
        out_shape=jax.ShapeDtypeStruct(x.shape, F32),
        compiler_params=pltpu.CompilerParams(dimension_semantics=("arbitrary", "arbitrary"),
                                             vmem_limit_bytes=V7X_VMEM_LIMIT_BYTES),
        name="mixer_out",
    )(dft, pq, x, ga, mb, w_a, w_out)


def _mlp_body(x_ref, g_ref, w_up_ref, w_down_ref, g_final_ref, o_ref, *, final_norm):
    x = x_ref[...]
    hb = _rms_norm(x, g_ref[...]).astype(BF16)
    acc = x
    for c0 in range(0, D_FF, FF_COLS_PER_DOT):
        f = jnp.maximum(_dot(hb, w_up_ref[:, c0:c0 + FF_COLS_PER_DOT]), 0.0)
        acc = acc + _dot((f * f).astype(BF16), w_down_ref[c0:c0 + FF_COLS_PER_DOT, :])
    if final_norm:
        acc = _rms_norm(acc, g_final_ref[...])
    o_ref[...] = acc


def _mlp(x2d, g, w_up, w_down, g_final, final_norm):
    n_tok = x2d.shape[0]
    t = TOKENS_PER_STEP
    tok = pl.BlockSpec((t, D_MODEL), lambda i: (i, 0))
    return pl.pallas_call(
        functools.partial(_mlp_body, final_norm=final_norm),
        grid=(n_tok // t,),
        in_specs=[tok, _resident((1, D_MODEL)), _resident((D_MODEL, D_FF)), _resident((D_FF, D_MODEL)),
                  _resident((1, D_MODEL))],
        out_specs=tok,
        out_shape=jax.ShapeDtypeStruct(x2d.shape, F32),
        compiler_params=pltpu.CompilerParams(dimension_semantics=("arbitrary",),
                                             vmem_limit_bytes=V7X_VMEM_LIMIT_BYTES),
        name="mlp",
    )(x2d, g, w_up, w_down, g_final)


def _channel_dft_table():
    n = FOURIER_GROUP_DIM
    idx = jnp.arange(D_FOURIER, dtype=jnp.int32)
    same_group = (idx[:, None] // n) == (idx[None, :] // n)
    ang = ((idx[:, None] % n) * (idx[None, :] % n) % n).astype(F32) * (2.0 * math.pi / n)
    scale = 1.0 / math.sqrt(n)
    cos = jnp.where(same_group, jnp.cos(ang) * scale, 0.0)
    sin = jnp.where(same_group, jnp.sin(ang) * scale, 0.0)
    return jnp.concatenate([cos, sin], axis=1).astype(BF16)


def _sequence_dft_table(seq):
    idx = jnp.arange(seq, dtype=jnp.int32)
    ang = ((idx[:, None] * idx[None, :]) % seq).astype(F32) * (2.0 * math.pi / seq)
    scale = 1.0 / math.sqrt(seq)
    return jnp.concatenate([jnp.cos(ang) * scale, -jnp.sin(ang) * scale], axis=1).astype(BF16)


def kernel(x, g_mix, w_in, w_a, ln_v_g, ln_v_b, w_s, b_s, w_b, w_out, g_mlp, w_up, w_down, g_final):
    bsz, seq, d = x.shape
    depth = w_in.shape[0]
    assert d == D_MODEL and seq % TOKENS_PER_STEP == 0 and TOKENS_PER_STEP % CHUNK == 0
    n_tok = bsz * seq

    cs = _channel_dft_table()
    dft = _sequence_dft_table(seq)
    row = lambda p: p.reshape(1, -1).astype(F32)
    g_final_row = row(g_final)

    for l in range(depth):
        bias = jnp.repeat(b_s[l].T.astype(F32), SGU_HEAD_DIM, axis=1)
        pq, ga, mb = _mixer_in(x.reshape(n_tok, d), row(g_mix[l]), w_in[l].astype(BF16), cs,
                               row(ln_v_g[l]), row(ln_v_b[l]), w_s[l].astype(BF16), bias,
                               w_b[l].astype(BF16))
        x = _mixer_out(dft, pq.reshape(bsz, seq, d), x, ga.reshape(bsz, seq, d), mb.reshape(bsz, seq, d),
                       w_a[l].astype(BF16), w_out[l].astype(BF16))
        x = _mlp(x.reshape(n_tok, d), row(g_mlp[l]), w_up[l].astype(BF16), w_down[l].astype(BF16),
                 g_final_row, final_norm=(l == depth - 1)).reshape(bsz, seq, d)
    return x
```

```python
import functools
import math

import jax
import jax.numpy as jnp
from jax import lax
from jax.experimental import pallas as pl
from jax.experimental.pallas import tpu as pltpu

D_MODEL = 1024
N_FOURIER_GROUPS = 8
D_FOURIER = D_MODEL // 2
FOURIER_GROUP_DIM = D_FOURIER // N_FOURIER_GROUPS
CHUNK = 128
N_SGU_HEADS = 8
D_SGU = D_MODEL
SGU_HEAD_DIM = D_SGU // N_SGU_HEADS
D_FF = 4 * D_MODEL
EPS = 1e-6

C_A = D_FOURIER
C_U = C_A + D_SGU
C_V = C_U + D_SGU
C_GA = C_V + D_MODEL
IN_COLS = C_GA + D_MODEL

V7X_VMEM_LIMIT_BYTES = 56 * 1024 * 1024

TOKENS_PER_STEP = 512
ROWS_PER_SUBBLOCK = 256
FF_COLS_PER_DOT = 1024

BF16 = jnp.bfloat16
F32 = jnp.float32


def _dot(a, b):
    return jnp.dot(a, b, preferred_element_type=F32)


def _rms_norm(x, g):
    return x * lax.rsqrt(jnp.mean(x * x, axis=-1, keepdims=True) + EPS) * g


def _gelu_tanh(x):
    c = math.sqrt(2.0 / math.pi)
    return 0.5 * x * (1.0 + jnp.tanh(c * (x + 0.044715 * (x * x * x))))


def _sigmoid(x):
    return 1.0 / (1.0 + jnp.exp(-x))


def _resident(shape):
    return pl.BlockSpec(shape, lambda *_: (0,) * len(shape), pipeline_mode=pl.Buffered(1))


def _mixer_in_rows(r0, nrows, x_ref, g_ref, w_in_ref, cs_ref, ln_g_ref, ln_b_ref, w_s_ref, bias_ref, w_b_ref,
                   pq_ref, ga_ref, mb_ref):
    rows = slice(r0, r0 + nrows)
    hb = _rms_norm(x_ref[rows, :], g_ref[...]).astype(BF16)

    v = _gelu_tanh(_dot(hb, w_in_ref[:, C_U:C_V]))
    mu = jnp.mean(v, axis=-1, keepdims=True)
    vc = v - mu
    vn = vc * lax.rsqrt(jnp.mean(vc * vc, axis=-1, keepdims=True) + EPS)
    vn = (vn * ln_g_ref[...] + ln_b_ref[...]).astype(BF16)
    u = _gelu_tanh(_dot(hb, w_in_ref[:, C_A:C_U]))

    a = _dot(hb, w_in_ref[:, 0:C_A]).astype(BF16)
    pq_ref[rows, :] = _dot(a, cs_ref[...]).astype(BF16)
    ga_ref[rows, :] = _sigmoid(_dot(hb, w_in_ref[:, C_V:C_GA])).astype(BF16)
    gb = _sigmoid(_dot(hb, w_in_ref[:, C_GA:IN_COLS]))

    mixed_rows = []
    for c in range(nrows // CHUNK):
        c_r0 = c * CHUNK
        cols = []
        for hd in range(N_SGU_HEADS):
            c0 = hd * SGU_HEAD_DIM
            cols.append(_dot(w_s_ref[hd], vn[c_r0:c_r0 + CHUNK, c0:c0 + SGU_HEAD_DIM]))
        mixed_rows.append(jnp.concatenate(cols, axis=1) + bias_ref[...])
    mixed = mixed_rows[0] if len(mixed_rows) == 1 else jnp.concatenate(mixed_rows, axis=0)
    s = (u * mixed).astype(BF16)
    mb_ref[rows, :] = (gb * _dot(s, w_b_ref[...])).astype(BF16)


def _mixer_in_body(*refs):
    for r0 in range(0, TOKENS_PER_STEP, ROWS_PER_SUBBLOCK):
        _mixer_in_rows(r0, ROWS_PER_SUBBLOCK, *refs)


def _mixer_in(x2d, g, w_in, cs, ln_g, ln_b, w_s, bias, w_b):
    n_tok = x2d.shape[0]
    t = TOKENS_PER_STEP
    tok = lambda width: pl.BlockSpec((t, width), lambda i: (i, 0))
    out = jax.ShapeDtypeStruct((n_tok, D_MODEL), BF16)
    return pl.pallas_call(
        _mixer_in_body,
        grid=(n_tok // t,),
        in_specs=[tok(D_MODEL), _resident((1, D_MODEL)), _resident((D_MODEL, IN_COLS)),
                  _resident((D_FOURIER, 2 * D_FOURIER)), _resident((1, D_SGU)), _resident((1, D_SGU)),
                  _resident((N_SGU_HEADS, CHUNK, CHUNK)), _resident((CHUNK, D_SGU)),
                  _resident((D_SGU, D_MODEL))],
        out_specs=[tok(2 * D_FOURIER), tok(D_MODEL), tok(D_MODEL)],
        out_shape=[out, out, out],
        compiler_params=pltpu.CompilerParams(dimension_semantics=("arbitrary",),
                                             vmem_limit_bytes=V7X_VMEM_LIMIT_BYTES),
        name="mixer_in",
    )(x2d, g, w_in, cs, ln_g, ln_b, w_s, bias, w_b)


def _mixer_out_body(dft_ref, pq_ref, x_ref, ga_ref, mb_ref, w_a_ref, w_out_ref, o_ref):
    seq = pq_ref.shape[1]
    y = _dot(dft_ref[:, 0:seq], pq_ref[0, :, 0:D_FOURIER])
    y = y + _dot(dft_ref[:, seq:2 * seq], pq_ref[0, :, D_FOURIER:2 * D_FOURIER])
    ya = _dot(y.astype(BF16), w_a_ref[...])
    merged = ga_ref[0].astype(F32) * ya + mb_ref[0].astype(F32)
    o_ref[0] = x_ref[0] + _dot(merged.astype(BF16), w_out_ref[...])


def _mixer_out(dft, pq, x, ga, mb, w_a, w_out):
    bsz, seq, _ = x.shape
    t = TOKENS_PER_STEP
    tok = lambda width: pl.BlockSpec((1, t, width), lambda b, i: (b, i, 0))
    return pl.pallas_call(
        _mixer_out_body,
        grid=(bsz, seq // t),
        in_specs=[pl.BlockSpec((t, 2 * seq), lambda b, i: (i, 0)),
                  pl.BlockSpec((1, seq, 2 * D_FOURIER), lambda b, i: (b, 0, 0)),
                  tok(D_MODEL), tok(D_MODEL), tok(D_MODEL),
                  _resident((D_FOURIER, D_MODEL)), _resident((D_MODEL, D_MODEL))],
        out_specs=tok(D_MODEL),
        out_shape=jax.ShapeDtypeStruct(x.shape, F32),
        compiler_params=pltpu.CompilerParams(dimension_semantics=("arbitrary", "arbitrary"),
                                             vmem_limit_bytes=V7X_VMEM_LIMIT_BYTES),
        name="mixer_out",
    )(dft, pq, x, ga, mb, w_a, w_out)


def _mlp_body(x_ref, g_ref, w_up_ref, w_down_ref, g_final_ref, o_ref, *, final_norm):
    x = x_ref[...]
    hb = _rms_norm(x, g_ref[...]).astype(BF16)
    acc = x
    for c0 in range(0, D_FF, FF_COLS_PER_DOT):
        f = jnp.maximum(_dot(hb, w_up_ref[:, c0:c0 + FF_COLS_PER_DOT]), 0.0)
        acc = acc + _dot((f * f).astype(BF16), w_down_ref[c0:c0 + FF_COLS_PER_DOT, :])
    if final_norm:
        acc = _rms_norm(acc, g_final_ref[...])
    o_ref[...] = acc


def _mlp(x2d, g, w_up, w_down, g_final, final_norm):
    n_tok = x2d.shape[0]
    t = TOKENS_PER_STEP
    tok = pl.BlockSpec((t, D_MODEL), lambda i: (i, 0))
    return pl.pallas_call(
        functools.partial(_mlp_body, final_norm=final_norm),
        grid=(n_tok // t,),
        in_specs=[tok, _resident((1, D_MODEL)), _resident((D_MODEL, D_FF)), _resident((D_FF, D_MODEL)),
                  _resident((1, D_MODEL))],
        out_specs=tok,
        out_shape=jax.ShapeDtypeStruct(x2d.shape, F32),
        compiler_params=pltpu.CompilerParams(dimension_semantics=("arbitrary",),
                                             vmem_limit_bytes=V7X_VMEM_LIMIT_BYTES),
        name="mlp",
    )(x2d, g, w_up, w_down, g_final)


def _channel_dft_table():
    n = FOURIER_GROUP_DIM
    idx = jnp.arange(D_FOURIER, dtype=jnp.int32)
    same_group = (idx[:, None] // n) == (idx[None, :] // n)
    ang = ((idx[:, None] % n) * (idx[None, :] % n) % n).astype(F32) * (2.0 * math.pi / n)
    scale = 1.0 / math.sqrt(n)
    cos = jnp.where(same_group, jnp.cos(ang) * scale, 0.0)
    sin = jnp.where(same_group, jnp.sin(ang) * scale, 0.0)
    return jnp.concatenate([cos, sin], axis=1).astype(BF16)


def _sequence_dft_table(seq):
    idx = jnp.arange(seq, dtype=jnp.int32)
    ang = ((idx[:, None] * idx[None, :]) % seq).astype(F32) * (2.0 * math.pi / seq)
    scale = 1.0 / math.sqrt(seq)
    return jnp.concatenate([jnp.cos(ang) * scale, -jnp.sin(ang) * scale], axis=1).astype(BF16)


def kernel(x, g_mix, w_in, w_a, ln_v_g, ln_v_b, w_s, b_s, w_b, w_out, g_mlp, w_up, w_down, g_final):
    bsz, seq, d = x.shape
    depth = w_in.shape[0]
    assert d == D_MODEL and seq % TOKENS_PER_STEP == 0 and TOKENS_PER_STEP % CHUNK == 0
    n_tok = bsz * seq

    cs = _channel_dft_table()
    dft = _sequence_dft_table(seq)
    row = lambda p: p.reshape(1, -1).astype(F32)
    g_final_row = row(g_final)

    for l in range(depth):
        bias = jnp.repeat(b_s[l].T.astype(F32), SGU_HEAD_DIM, axis=1)
        pq, ga, mb = _mixer_in(x.reshape(n_tok, d), row(g_mix[l]), w_in[l].astype(BF16), cs,
                               row(ln_v_g[l]), row(ln_v_b[l]), w_s[l].astype(BF16), bias,
                               w_b[l].astype(BF16))
        x = _mixer_out(dft, pq.reshape(bsz, seq, d), x, ga.reshape(bsz, seq, d), mb.reshape(bsz, seq, d),
                       w_a[l].astype(BF16), w_out[l].astype(BF16))
        x = _mlp(x.reshape(n_tok, d), row(g_mlp[l]), w_up[l].astype(BF16), w_down[l].astype(BF16),
                 g_final_row, final_norm=(l == depth - 1)).reshape(bsz, seq, d)
    return x
```

```python
import functools
import math

import jax
import jax.numpy as jnp
from jax import lax
from jax.experimental import pallas as pl
from jax.experimental.pallas import tpu as pltpu

D_MODEL = 1024
N_FOURIER_GROUPS = 8
D_FOURIER = D_MODEL // 2
FOURIER_GROUP_DIM = D_FOURIER // N_FOURIER_GROUPS
CHUNK = 128
N_SGU_HEADS = 8
D_SGU = D_MODEL
SGU_HEAD_DIM = D_SGU // N_SGU_HEADS
D_FF = 4 * D_MODEL
EPS = 1e-6

C_A = D_FOURIER
C_U = C_A + D_SGU
C_V = C_U + D_SGU
C_GA = C_V + D_MODEL
IN_COLS = C_GA + D_MODEL

V7X_VMEM_LIMIT_BYTES = 56 * 1024 * 1024

TOKENS_PER_STEP = 512
ROWS_PER_SUBBLOCK = 256
FF_COLS_PER_DOT = 1024

BF16 = jnp.bfloat16
F32 = jnp.float32


def _dot(a, b):
    return jnp.dot(a, b, preferred_element_type=F32)


def _rms_norm(x, g):
    return x * lax.rsqrt(jnp.mean(x * x, axis=-1, keepdims=True) + EPS) * g


def _gelu_tanh(x):
    c = math.sqrt(2.0 / math.pi)
    return 0.5 * x * (1.0 + jnp.tanh(c * (x + 0.044715 * (x * x * x))))


def _sigmoid(x):
    return 1.0 / (1.0 + jnp.exp(-x))


def _resident(shape):
    return pl.BlockSpec(shape, lambda *_: (0,) * len(shape), pipeline_mode=pl.Buffered(1))


def _mixer_in_rows(r0, nrows, x_ref, g_ref, w_in_ref, cs_ref, ln_g_ref, ln_b_ref, w_s_ref, bias_ref, w_b_ref,
                   pq_ref, ga_ref, mb_ref):
    rows = slice(r0, r0 + nrows)
    hb = _rms_norm(x_ref[rows, :], g_ref[...]).astype(BF16)

    v = _gelu_tanh(_dot(hb, w_in_ref[:, C_U:C_V]))
    mu = jnp.mean(v, axis=-1, keepdims=True)
    vc = v - mu
    vn = vc * lax.rsqrt(jnp.mean(vc * vc, axis=-1, keepdims=True) + EPS)
    vn = (vn * ln_g_ref[...] + ln_b_ref[...]).astype(BF16)
    u = _gelu_tanh(_dot(hb, w_in_ref[:, C_A:C_U]))

    a = _dot(hb, w_in_ref[:, 0:C_A]).astype(BF16)
    pq_ref[rows, :] = _dot(a, cs_ref[...]).astype(BF16)
    ga_ref[rows, :] = _sigmoid(_dot(hb, w_in_ref[:, C_V:C_GA])).astype(BF16)
    gb = _sigmoid(_dot(hb, w_in_ref[:, C_GA:IN_COLS]))

    mixed_rows = []
    for c in range(nrows // CHUNK):
        c_r0 = c * CHUNK
        cols = []
        for hd in range(N_SGU_HEADS):
            c0 = hd * SGU_HEAD_DIM
            cols.append(_dot(w_s_ref[hd], vn[c_r0:c_r0 + CHUNK, c0:c0 + SGU_HEAD_DIM]))
        mixed_rows.append(jnp.concatenate(cols, axis=1) + bias_ref[...])
    mixed = mixed_rows[0] if len(mixed_rows) == 1 else jnp.concatenate(mixed_rows, axis=0)
    s = (u * mixed).astype(BF16)
    mb_ref[rows, :] = (gb * _dot(s, w_b_ref[...])).astype(BF16)


def _mixer_in_body(*refs):
    for r0 in range(0, TOKENS_PER_STEP, ROWS_PER_SUBBLOCK):
        _mixer_in_rows(r0, ROWS_PER_SUBBLOCK, *refs)


def _mixer_in(x2d, g, w_in, cs, ln_g, ln_b, w_s, bias, w_b):
    n_tok = x2d.shape[0]
    t = TOKENS_PER_STEP
    tok = lambda width: pl.BlockSpec((t, width), lambda i: (i, 0))
    out = jax.ShapeDtypeStruct((n_tok, D_MODEL), BF16)
    return pl.pallas_call(
        _mixer_in_body,
        grid=(n_tok // t,),
        in_specs=[tok(D_MODEL), _resident((1, D_MODEL)), _resident((D_MODEL, IN_COLS)),
                  _resident((D_FOURIER, 2 * D_FOURIER)), _resident((1, D_SGU)), _resident((1, D_SGU)),
                  _resident((N_SGU_HEADS, CHUNK, CHUNK)), _resident((CHUNK, D_SGU)),
                  _resident((D_SGU, D_MODEL))],
        out_specs=[tok(2 * D_FOURIER), tok(D_MODEL), tok(D_MODEL)],
        out_shape=[out, out, out],
        compiler_params=pltpu.CompilerParams(dimension_semantics=("arbitrary",),
                                             vmem_limit_bytes=V7X_VMEM_LIMIT_BYTES),
        name="mixer_in",
    )(x2d, g, w_in, cs, ln_g, ln_b, w_s, bias, w_b)


def _tail_rows(r0, nrows, dft_ref, pq_ref, x_ref, ga_ref, mb_ref, w_a_ref, w_out_ref, g_ref, w_up_ref,
               w_down_ref, g_final_ref, o_ref, *, final_norm):
    rows = slice(r0, r0 + nrows)
    seq = pq_ref.shape[1]
    y = _dot(dft_ref[rows, 0:seq], pq_ref[0, :, 0:D_FOURIER])
    y = y + _dot(dft_ref[rows, seq:2 * seq], pq_ref[0, :, D_FOURIER:2 * D_FOURIER])
    ya = _dot(y.astype(BF16), w_a_ref[...])
    merged = ga_ref[0, rows, :].astype(F32) * ya + mb_ref[0, rows, :].astype(F32)
    x = x_ref[0, rows, :] + _dot(merged.astype(BF16), w_out_ref[...])
    hb = _rms_norm(x, g_ref[...]).astype(BF16)
    acc = x
    for c0 in range(0, D_FF, FF_COLS_PER_DOT):
        f = jnp.maximum(_dot(hb, w_up_ref[:, c0:c0 + FF_COLS_PER_DOT]), 0.0)
        acc = acc + _dot((f * f).astype(BF16), w_down_ref[c0:c0 + FF_COLS_PER_DOT, :])
    if final_norm:
        acc = _rms_norm(acc, g_final_ref[...])
    o_ref[0, rows, :] = acc


def _tail_body(*refs, final_norm):
    for r0 in range(0, TOKENS_PER_STEP, ROWS_PER_SUBBLOCK):
        _tail_rows(r0, ROWS_PER_SUBBLOCK, *refs, final_norm=final_norm)


def _tail(dft, pq, x, ga, mb, w_a, w_out, g, w_up, w_down, g_final, final_norm):
    bsz, seq, _ = x.shape
    t = TOKENS_PER_STEP
    tok = lambda width: pl.BlockSpec((1, t, width), lambda i, b: (b, i, 0))
    return pl.pallas_call(
        functools.partial(_tail_body, final_norm=final_norm),
        grid=(seq // t, bsz),
        in_specs=[pl.BlockSpec((t, 2 * seq), lambda i, b: (i, 0), pipeline_mode=pl.Buffered(1)),
                  pl.BlockSpec((1, seq, 2 * D_FOURIER), lambda i, b: (b, 0, 0)),
                  tok(D_MODEL), tok(D_MODEL), tok(D_MODEL),
                  _resident((D_FOURIER, D_MODEL)), _resident((D_MODEL, D_MODEL)), _resident((1, D_MODEL)),
                  _resident((D_MODEL, D_FF)), _resident((D_FF, D_MODEL)), _resident((1, D_MODEL))],
        out_specs=tok(D_MODEL),
        out_shape=jax.ShapeDtypeStruct(x.shape, F32),
        compiler_params=pltpu.CompilerParams(dimension_semantics=("arbitrary", "arbitrary"),
                                             vmem_limit_bytes=V7X_VMEM_LIMIT_BYTES),
        name="tail",
    )(dft, pq, x, ga, mb, w_a, w_out, g, w_up, w_down, g_final)


def _channel_dft_table():
    n = FOURIER_GROUP_DIM
    idx = jnp.arange(D_FOURIER, dtype=jnp.int32)
    same_group = (idx[:, None] // n) == (idx[None, :] // n)
    ang = ((idx[:, None] % n) * (idx[None, :] % n) % n).astype(F32) * (2.0 * math.pi / n)
    scale = 1.0 / math.sqrt(n)
    cos = jnp.where(same_group, jnp.cos(ang) * scale, 0.0)
    sin = jnp.where(same_group, jnp.sin(ang) * scale, 0.0)
    return jnp.concatenate([cos, sin], axis=1).astype(BF16)


def _sequence_dft_table(seq):
    idx = jnp.arange(seq, dtype=jnp.int32)
    ang = ((idx[:, None] * idx[None, :]) % seq).astype(F32) * (2.0 * math.pi / seq)
    scale = 1.0 / math.sqrt(seq)
    return jnp.concatenate([jnp.cos(ang) * scale, -jnp.sin(ang) * scale], axis=1).astype(BF16)


def kernel(x, g_mix, w_in, w_a, ln_v_g, ln_v_b, w_s, b_s, w_b, w_out, g_mlp, w_up, w_down, g_final):
    bsz, seq, d = x.shape
    depth = w_in.shape[0]
    assert d == D_MODEL and seq % TOKENS_PER_STEP == 0 and TOKENS_PER_STEP % CHUNK == 0
    n_tok = bsz * seq

    cs = _channel_dft_table()
    dft = _sequence_dft_table(seq)
    row = lambda p: p.reshape(1, -1).astype(F32)
    g_final_row = row(g_final)

    for l in range(depth):
        bias = jnp.repeat(b_s[l].T.astype(F32), SGU_HEAD_DIM, axis=1)
        pq, ga, mb = _mixer_in(x.reshape(n_tok, d), row(g_mix[l]), w_in[l].astype(BF16), cs,
                               row(ln_v_g[l]), row(ln_v_b[l]), w_s[l].astype(BF16), bias,
                               w_b[l].astype(BF16))
        x = _tail(dft, pq.reshape(bsz, seq, d), x, ga.reshape(bsz, seq, d), mb.reshape(bsz, seq, d),
                  w_a[l].astype(BF16), w_out[l].astype(BF16), row(g_mlp[l]), w_up[l].astype(BF16),
                  w_down[l].astype(BF16), g_final_row, final_norm=(l == depth - 1))
    return x
```

```python
import functools
import math

import jax
import jax.numpy as jnp
from jax import lax
from jax.experimental import pallas as pl
from jax.experimental.pallas import tpu as pltpu

D_MODEL = 1024
N_FOURIER_GROUPS = 8
D_FOURIER = D_MODEL // 2
FOURIER_GROUP_DIM = D_FOURIER // N_FOURIER_GROUPS
CHUNK = 128
N_SGU_HEADS = 8
D_SGU = D_MODEL
SGU_HEAD_DIM = D_SGU // N_SGU_HEADS
D_FF = 4 * D_MODEL
EPS = 1e-6

C_A = D_FOURIER
C_U = C_A + D_SGU
C_V = C_U + D_SGU
C_GA = C_V + D_MODEL
IN_COLS = C_GA + D_MODEL

V7X_VMEM_LIMIT_BYTES = 56 * 1024 * 1024

TOKENS_PER_STEP = 512
ROWS_PER_SUBBLOCK = 256
FF_COLS_PER_DOT = 1024
REVERSE_ROWS = 256

BF16 = jnp.bfloat16
F32 = jnp.float32


def _dot(a, b):
    return jnp.dot(a, b, preferred_element_type=F32)


def _rms_norm(x, g):
    return x * lax.rsqrt(jnp.mean(x * x, axis=-1, keepdims=True) + EPS) * g


def _gelu_tanh(x):
    c = math.sqrt(2.0 / math.pi)
    return 0.5 * x * (1.0 + jnp.tanh(c * (x + 0.044715 * (x * x * x))))


def _sigmoid(x):
    return 1.0 / (1.0 + jnp.exp(-x))


def _resident(shape):
    return pl.BlockSpec(shape, lambda *_: (0,) * len(shape), pipeline_mode=pl.Buffered(1))


def _mixer_in_rows(r0, nrows, x_ref, g_ref, w_in_ref, ln_g_ref, ln_b_ref, w_s_ref, bias_ref, w_b_ref,
                   a_ref, ga_ref, mb_ref):
    rows = slice(r0, r0 + nrows)
    hb = _rms_norm(x_ref[rows, :], g_ref[...]).astype(BF16)

    v = _gelu_tanh(_dot(hb, w_in_ref[:, C_U:C_V]))
    mu = jnp.mean(v, axis=-1, keepdims=True)
    vc = v - mu
    vn = vc * lax.rsqrt(jnp.mean(vc * vc, axis=-1, keepdims=True) + EPS)
    vn = (vn * ln_g_ref[...] + ln_b_ref[...]).astype(BF16)
    u = _gelu_tanh(_dot(hb, w_in_ref[:, C_A:C_U]))

    a_ref[rows, :] = _dot(hb, w_in_ref[:, 0:C_A]).astype(BF16)
    ga_ref[rows, :] = _sigmoid(_dot(hb, w_in_ref[:, C_V:C_GA])).astype(BF16)
    gb = _sigmoid(_dot(hb, w_in_ref[:, C_GA:IN_COLS]))

    mixed_rows = []
    for c in range(nrows // CHUNK):
        c_r0 = c * CHUNK
        cols = []
        for hd in range(N_SGU_HEADS):
            c0 = hd * SGU_HEAD_DIM
            cols.append(_dot(w_s_ref[hd], vn[c_r0:c_r0 + CHUNK, c0:c0 + SGU_HEAD_DIM]))
        mixed_rows.append(jnp.concatenate(cols, axis=1) + bias_ref[...])
    mixed = mixed_rows[0] if len(mixed_rows) == 1 else jnp.concatenate(mixed_rows, axis=0)
    s = (u * mixed).astype(BF16)
    mb_ref[rows, :] = (gb * _dot(s, w_b_ref[...])).astype(BF16)


def _mixer_in_body(*refs):
    for r0 in range(0, TOKENS_PER_STEP, ROWS_PER_SUBBLOCK):
        _mixer_in_rows(r0, ROWS_PER_SUBBLOCK, *refs)


def _mixer_in(x2d, g, w_in, ln_g, ln_b, w_s, bias, w_b):
    n_tok = x2d.shape[0]
    t = TOKENS_PER_STEP
    tok = lambda width: pl.BlockSpec((t, width), lambda i: (i, 0))
    out = lambda width: jax.ShapeDtypeStruct((n_tok, width), BF16)
    return pl.pallas_call(
        _mixer_in_body,
        grid=(n_tok // t,),
        in_specs=[tok(D_MODEL), _resident((1, D_MODEL)), _resident((D_MODEL, IN_COLS)),
                  _resident((1, D_SGU)), _resident((1, D_SGU)),
                  _resident((N_SGU_HEADS, CHUNK, CHUNK)), _resident((CHUNK, D_SGU)),
                  _resident((D_SGU, D_MODEL))],
        out_specs=[tok(D_FOURIER), tok(D_MODEL), tok(D_MODEL)],
        out_shape=[out(D_FOURIER), out(D_MODEL), out(D_MODEL)],
        compiler_params=pltpu.CompilerParams(dimension_semantics=("arbitrary",),
                                             vmem_limit_bytes=V7X_VMEM_LIMIT_BYTES),
        name="mixer_in",
    )(x2d, g, w_in, ln_g, ln_b, w_s, bias, w_b)


def _reverse_shifted(rev_ref, block, first_row):
    y = _dot(rev_ref[...], block)
    return jnp.where(lax.broadcasted_iota(jnp.int32, y.shape, 0) == 0, first_row, y)


def _fourier_body(a_ref, cos_ref, sin_ref, rev_ref, uv_ref):
    seq, width = a_ref.shape[1], a_ref.shape[2]
    half = seq // 2
    r = REVERSE_ROWS
    nb = half // r
    inv_sqrt_n = 1.0 / math.sqrt(seq)
    zero_row = jnp.zeros((1, width), F32)

    def bottom_block(q):
        return a_ref[0, half + q * r:half + (q + 1) * r, :]

    even, odd = [], []
    for p in range(nb):
        top = a_ref[0, p * r:(p + 1) * r, :].astype(F32)
        first = zero_row if p == 0 else bottom_block(nb - p)[0:1, :].astype(F32)
        mirrored = _reverse_shifted(rev_ref, bottom_block(nb - 1 - p), first)
        even.append((top + mirrored).astype(BF16))
        odd.append((top - mirrored).astype(BF16))
    a_even = jnp.concatenate(even, axis=0)
    a_odd = jnp.concatenate(odd, axis=0)

    nyquist = bottom_block(0)[0:1, :].astype(F32)
    sign = (1 - 2 * (lax.broadcasted_iota(jnp.int32, (half, 1), 0) & 1)).astype(F32)
    u = _dot(cos_ref[...], a_even) + sign * (nyquist * inv_sqrt_n)
    v = _dot(sin_ref[...], a_odd)
    uv_ref[0, 0:half, :] = jnp.concatenate([u, v], axis=1).astype(BF16)

    u_half = (jnp.sum(a_even.astype(F32) * sign, axis=0, keepdims=True) + nyquist) * inv_sqrt_n
    z = jnp.concatenate([u, -v], axis=1).astype(BF16)
    for p in range(nb):
        if p == 0:
            first = jnp.concatenate([u_half, zero_row], axis=1)
        else:
            first = z[(nb - p) * r:(nb - p) * r + 1, :].astype(F32)
        block = _reverse_shifted(rev_ref, z[(nb - 1 - p) * r:(nb - p) * r, :], first)
        uv_ref[0, half + p * r:half + (p + 1) * r, :] = block.astype(BF16)


def _fourier(a, cos_half, sin_half, rev):
    bsz, seq, width = a.shape
    half = seq // 2
    return pl.pallas_call(
        _fourier_body,
        grid=(bsz,),
        in_specs=[pl.BlockSpec((1, seq, width), lambda b: (b, 0, 0)),
                  _resident((half, half)), _resident((half, half)), _resident((REVERSE_ROWS, REVERSE_ROWS))],
        out_specs=pl.BlockSpec((1, seq, 2 * width), lambda b: (b, 0, 0)),
        out_shape=jax.ShapeDtypeStruct((bsz, seq, 2 * width), BF16),
        compiler_params=pltpu.CompilerParams(dimension_semantics=("arbitrary",),
                                             vmem_limit_bytes=V7X_VMEM_LIMIT_BYTES),
        name="fourier",
    )(a, cos_half, sin_half, rev)


def _fold_w_a_body(cs_ref, w_a_ref, o_ref):
    o_ref[...] = _dot(cs_ref[...], w_a_ref[...]).astype(BF16)


def _fold_w_a(cs, w_a):
    return pl.pallas_call(
        _fold_w_a_body,
        out_shape=jax.ShapeDtypeStruct((cs.shape[0], w_a.shape[1]), BF16),
        name="fold_w_a",
    )(cs, w_a)


def _tail_rows(r0, nrows, uv_ref, x_ref, ga_ref, mb_ref, w_uv_ref, w_out_ref, g_ref, w_up_ref, w_down_ref,
               g_final_ref, o_ref, *, final_norm):
    rows = slice(r0, r0 + nrows)
    ya = _dot(uv_ref[rows, :], w_uv_ref[...])
    merged = ga_ref[rows, :].astype(F32) * ya + mb_ref[rows, :].astype(F32)
    x = x_ref[rows, :] + _dot(merged.astype(BF16), w_out_ref[...])
    hb = _rms_norm(x, g_ref[...]).astype(BF16)
    acc = x
    for c0 in range(0, D_FF, FF_COLS_PER_DOT):
        f = jnp.maximum(_dot(hb, w_up_ref[:, c0:c0 + FF_COLS_PER_DOT]), 0.0)
        acc = acc + _dot((f * f).astype(BF16), w_down_ref[c0:c0 + FF_COLS_PER_DOT, :])
    if final_norm:
        acc = _rms_norm(acc, g_final_ref[...])
    o_ref[rows, :] = acc


def _tail_body(*refs, final_norm):
    for r0 in range(0, TOKENS_PER_STEP, ROWS_PER_SUBBLOCK):
        _tail_rows(r0, ROWS_PER_SUBBLOCK, *refs, final_norm=final_norm)


def _tail(uv, x2d, ga, mb, w_uv, w_out, g, w_up, w_down, g_final, final_norm):
    n_tok = x2d.shape[0]
    t = TOKENS_PER_STEP
    tok = pl.BlockSpec((t, D_MODEL), lambda i: (i, 0))
    return pl.pallas_call(
        functools.partial(_tail_body, final_norm=final_norm),
        grid=(n_tok // t,),
        in_specs=[tok, tok, tok, tok,
                  _resident((2 * D_FOURIER, D_MODEL)), _resident((D_MODEL, D_MODEL)), _resident((1, D_MODEL)),
                  _resident((D_MODEL, D_FF)), _resident((D_FF, D_MODEL)), _resident((1, D_MODEL))],
        out_specs=tok,
        out_shape=jax.ShapeDtypeStruct(x2d.shape, F32),
        compiler_params=pltpu.CompilerParams(dimension_semantics=("arbitrary",),
                                             vmem_limit_bytes=V7X_VMEM_LIMIT_BYTES),
        name="tail",
    )(uv, x2d, ga, mb, w_uv, w_out, g, w_up, w_down, g_final)


def _channel_dft_table():
    n = FOURIER_GROUP_DIM
    idx = jnp.arange(D_FOURIER, dtype=jnp.int32)
    same_group = (idx[:, None] // n) == (idx[None, :] // n)
    ang = ((idx[:, None] % n) * (idx[None, :] % n) % n).astype(F32) * (2.0 * math.pi / n)
    scale = 1.0 / math.sqrt(n)
    cos = jnp.where(same_group, jnp.cos(ang) * scale, 0.0)
    sin = jnp.where(same_group, jnp.sin(ang) * scale, 0.0)
    return jnp.concatenate([cos, -sin], axis=0).astype(BF16)


def _sequence_dft_tables(seq):
    idx = jnp.arange(seq // 2, dtype=jnp.int32)
    ang = ((idx[:, None] * idx[None, :]) % seq).astype(F32) * (2.0 * math.pi / seq)
    scale = 1.0 / math.sqrt(seq)
    return (jnp.cos(ang) * scale).astype(BF16), (jnp.sin(ang) * scale).astype(BF16)


def _reverse_table():
    idx = jnp.arange(REVERSE_ROWS, dtype=jnp.int32)
    return (idx[None, :] == REVERSE_ROWS - idx[:, None]).astype(BF16)


def kernel(x, g_mix, w_in, w_a, ln_v_g, ln_v_b, w_s, b_s, w_b, w_out, g_mlp, w_up, w_down, g_final):
    bsz, seq, d = x.shape
    depth = w_in.shape[0]
    assert d == D_MODEL and seq % TOKENS_PER_STEP == 0 and TOKENS_PER_STEP % ROWS_PER_SUBBLOCK == 0
    assert ROWS_PER_SUBBLOCK % CHUNK == 0 and (seq // 2) % REVERSE_ROWS == 0
    n_tok = bsz * seq

    cs = _channel_dft_table()
    cos_half, sin_half = _sequence_dft_tables(seq)
    rev = _reverse_table()
    row = lambda p: p.reshape(1, -1).astype(F32)
    g_final_row = row(g_final)

    x = x.reshape(n_tok, d)
    for l in range(depth):
        bias = jnp.repeat(b_s[l].T.astype(F32), SGU_HEAD_DIM, axis=1)
        a, ga, mb = _mixer_in(x, row(g_mix[l]), w_in[l].astype(BF16), row(ln_v_g[l]), row(ln_v_b[l]),
                              w_s[l].astype(BF16), bias, w_b[l].astype(BF16))
        uv = _fourier(a.reshape(bsz, seq, D_FOURIER), cos_half, sin_half, rev)
        x = _tail(uv.reshape(n_tok, d), x, ga, mb, _fold_w_a(cs, w_a[l].astype(BF16)), w_out[l].astype(BF16),
                  row(g_mlp[l]), w_up[l].astype(BF16), w_down[l].astype(BF16), g_final_row,
                  final_norm=(l == depth - 1))
    return x.reshape(bsz, seq, d)
```

```python
import functools
import math

import jax
import jax.numpy as jnp
from jax import lax
from jax.experimental import pallas as pl
from jax.experimental.pallas import tpu as pltpu

D_MODEL = 1024
N_FOURIER_GROUPS = 8
D_FOURIER = D_MODEL // 2
FOURIER_GROUP_DIM = D_FOURIER // N_FOURIER_GROUPS
CHUNK = 128
N_SGU_HEADS = 8
D_SGU = D_MODEL
SGU_HEAD_DIM = D_SGU // N_SGU_HEADS
D_FF = 4 * D_MODEL
EPS = 1e-6

C_A = D_FOURIER
C_U = C_A + D_SGU
C_V = C_U + D_SGU
C_GA = C_V + D_MODEL
IN_COLS = C_GA + D_MODEL

V7X_VMEM_LIMIT_BYTES = 56 * 1024 * 1024

TOKENS_PER_STEP = 1024
ROWS_PER_SUBBLOCK = 256
FF_COLS_PER_DOT = 1024
REVERSE_ROWS = 256

BF16 = jnp.bfloat16
F32 = jnp.float32


def _dot(a, b):
    return jnp.dot(a, b, preferred_element_type=F32)


def _rms_norm(x, g):
    return x * lax.rsqrt(jnp.mean(x * x, axis=-1, keepdims=True) + EPS) * g


def _gelu_tanh(x):
    c = math.sqrt(2.0 / math.pi)
    return 0.5 * x * (1.0 + jnp.tanh(c * (x + 0.044715 * (x * x * x))))


def _sigmoid(x):
    return 1.0 / (1.0 + jnp.exp(-x))


def _resident(shape):
    return pl.BlockSpec(shape, lambda *_: (0,) * len(shape), pipeline_mode=pl.Buffered(1))


def _mixer_in_rows(r0, nrows, x_ref, g_ref, w_in_ref, ln_g_ref, ln_b_ref, w_s_ref, bias_ref, w_b_ref,
                   a_ref, ga_ref, mb_ref):
    rows = slice(r0, r0 + nrows)
    hb = _rms_norm(x_ref[rows, :], g_ref[...]).astype(BF16)

    v = _gelu_tanh(_dot(hb, w_in_ref[:, C_U:C_V]))
    mu = jnp.mean(v, axis=-1, keepdims=True)
    vc = v - mu
    vn = vc * lax.rsqrt(jnp.mean(vc * vc, axis=-1, keepdims=True) + EPS)
    vn = (vn * ln_g_ref[...] + ln_b_ref[...]).astype(BF16)
    u = _gelu_tanh(_dot(hb, w_in_ref[:, C_A:C_U]))

    a_ref[rows, :] = _dot(hb, w_in_ref[:, 0:C_A]).astype(BF16)
    ga_ref[rows, :] = _sigmoid(_dot(hb, w_in_ref[:, C_V:C_GA])).astype(BF16)
    gb = _sigmoid(_dot(hb, w_in_ref[:, C_GA:IN_COLS]))

    mixed_rows = []
    for c in range(nrows // CHUNK):
        c_r0 = c * CHUNK
        cols = []
        for hd in range(N_SGU_HEADS):
            c0 = hd * SGU_HEAD_DIM
            cols.append(_dot(w_s_ref[hd], vn[c_r0:c_r0 + CHUNK, c0:c0 + SGU_HEAD_DIM]))
        mixed_rows.append(jnp.concatenate(cols, axis=1) + bias_ref[...])
    mixed = mixed_rows[0] if len(mixed_rows) == 1 else jnp.concatenate(mixed_rows, axis=0)
    s = (u * mixed).astype(BF16)
    mb_ref[rows, :] = (gb * _dot(s, w_b_ref[...])).astype(BF16)


def _mixer_in_body(*refs):
    for r0 in range(0, TOKENS_PER_STEP, ROWS_PER_SUBBLOCK):
        _mixer_in_rows(r0, ROWS_PER_SUBBLOCK, *refs)


def _mixer_in(x2d, g, w_in, ln_g, ln_b, w_s, bias, w_b):
    n_tok = x2d.shape[0]
    t = TOKENS_PER_STEP
    tok = lambda width: pl.BlockSpec((t, width), lambda i: (i, 0))
    out = lambda width: jax.ShapeDtypeStruct((n_tok, width), BF16)
    return pl.pallas_call(
        _mixer_in_body,
        grid=(n_tok // t,),
        in_specs=[tok(D_MODEL), _resident((1, D_MODEL)), _resident((D_MODEL, IN_COLS)),
                  _resident((1, D_SGU)), _resident((1, D_SGU)),
                  _resident((N_SGU_HEADS, CHUNK, CHUNK)), _resident((CHUNK, D_SGU)),
                  _resident((D_SGU, D_MODEL))],
        out_specs=[tok(D_FOURIER), tok(D_MODEL), tok(D_MODEL)],
        out_shape=[out(D_FOURIER), out(D_MODEL), out(D_MODEL)],
        compiler_params=pltpu.CompilerParams(dimension_semantics=("arbitrary",),
                                             vmem_limit_bytes=V7X_VMEM_LIMIT_BYTES),
        name="mixer_in",
    )(x2d, g, w_in, ln_g, ln_b, w_s, bias, w_b)


def _reverse_shifted(rev_ref, block, first_row):
    y = _dot(rev_ref[...], block)
    return jnp.where(lax.broadcasted_iota(jnp.int32, y.shape, 0) == 0, first_row, y)


def _fourier_body(a_ref, cos_ref, sin_ref, rev_ref, uv_ref):
    seq, width = a_ref.shape[1], a_ref.shape[2]
    half = seq // 2
    r = REVERSE_ROWS
    nb = half // r
    inv_sqrt_n = 1.0 / math.sqrt(seq)
    zero_row = jnp.zeros((1, width), F32)

    def bottom_block(q):
        return a_ref[0, half + q * r:half + (q + 1) * r, :]

    even, odd = [], []
    for p in range(nb):
        top = a_ref[0, p * r:(p + 1) * r, :].astype(F32)
        first = zero_row if p == 0 else bottom_block(nb - p)[0:1, :].astype(F32)
        mirrored = _reverse_shifted(rev_ref, bottom_block(nb - 1 - p), first)
        even.append((top + mirrored).astype(BF16))
        odd.append((top - mirrored).astype(BF16))
    a_even = jnp.concatenate(even, axis=0)
    a_odd = jnp.concatenate(odd, axis=0)

    nyquist = bottom_block(0)[0:1, :].astype(F32)
    sign = (1 - 2 * (lax.broadcasted_iota(jnp.int32, (half, 1), 0) & 1)).astype(F32)
    u = _dot(cos_ref[...], a_even) + sign * (nyquist * inv_sqrt_n)
    v = _dot(sin_ref[...], a_odd)
    uv_ref[0, 0:half, :] = jnp.concatenate([u, v], axis=1).astype(BF16)

    u_half = (jnp.sum(a_even.astype(F32) * sign, axis=0, keepdims=True) + nyquist) * inv_sqrt_n
    z = jnp.concatenate([u, -v], axis=1).astype(BF16)
    for p in range(nb):
        if p == 0:
            first = jnp.concatenate([u_half, zero_row], axis=1)
        else:
            first = z[(nb - p) * r:(nb - p) * r + 1, :].astype(F32)
        block = _reverse_shifted(rev_ref, z[(nb - 1 - p) * r:(nb - p) * r, :], first)
        uv_ref[0, half + p * r:half + (p + 1) * r, :] = block.astype(BF16)


def _fourier(a, cos_half, sin_half, rev):
    bsz, seq, width = a.shape
    half = seq // 2
    return pl.pallas_call(
        _fourier_body,
        grid=(bsz,),
        in_specs=[pl.BlockSpec((1, seq, width), lambda b: (b, 0, 0)),
                  _resident((half, half)), _resident((half, half)), _resident((REVERSE_ROWS, REVERSE_ROWS))],
        out_specs=pl.BlockSpec((1, seq, 2 * width), lambda b: (b, 0, 0)),
        out_shape=jax.ShapeDtypeStruct((bsz, seq, 2 * width), BF16),
        compiler_params=pltpu.CompilerParams(dimension_semantics=("arbitrary",),
                                             vmem_limit_bytes=V7X_VMEM_LIMIT_BYTES),
        name="fourier",
    )(a, cos_half, sin_half, rev)


def _fold_w_a_body(cs_ref, w_a_ref, o_ref):
    o_ref[...] = _dot(cs_ref[...], w_a_ref[...]).astype(BF16)


def _fold_w_a(cs, w_a):
    return pl.pallas_call(
        _fold_w_a_body,
        out_shape=jax.ShapeDtypeStruct((cs.shape[0], w_a.shape[1]), BF16),
        name="fold_w_a",
    )(cs, w_a)


def _tail_rows(r0, nrows, uv_ref, x_ref, ga_ref, mb_ref, w_uv_ref, w_out_ref, g_ref, w_up_ref, w_down_ref,
               g_final_ref, o_ref, *, final_norm):
    rows = slice(r0, r0 + nrows)
    ya = _dot(uv_ref[rows, :], w_uv_ref[...])
    merged = ga_ref[rows, :].astype(F32) * ya + mb_ref[rows, :].astype(F32)
    x = x_ref[rows, :] + _dot(merged.astype(BF16), w_out_ref[...])
    hb = _rms_norm(x, g_ref[...]).astype(BF16)
    acc = x
    for c0 in range(0, D_FF, FF_COLS_PER_DOT):
        f = jnp.maximum(_dot(hb, w_up_ref[:, c0:c0 + FF_COLS_PER_DOT]), 0.0)
        acc = acc + _dot((f * f).astype(BF16), w_down_ref[c0:c0 + FF_COLS_PER_DOT, :])
    if final_norm:
        acc = _rms_norm(acc, g_final_ref[...])
    o_ref[rows, :] = acc


def _tail_body(*refs, final_norm):
    for r0 in range(0, TOKENS_PER_STEP, ROWS_PER_SUBBLOCK):
        _tail_rows(r0, ROWS_PER_SUBBLOCK, *refs, final_norm=final_norm)


def _tail(uv, x2d, ga, mb, w_uv, w_out, g, w_up, w_down, g_final, final_norm):
    n_tok = x2d.shape[0]
    t = TOKENS_PER_STEP
    tok = pl.BlockSpec((t, D_MODEL), lambda i: (i, 0))
    return pl.pallas_call(
        functools.partial(_tail_body, final_norm=final_norm),
        grid=(n_tok // t,),
        in_specs=[tok, tok, tok, tok,
                  _resident((2 * D_FOURIER, D_MODEL)), _resident((D_MODEL, D_MODEL)), _resident((1, D_MODEL)),
                  _resident((D_MODEL, D_FF)), _resident((D_FF, D_MODEL)), _resident((1, D_MODEL))],
        out_specs=tok,
        out_shape=jax.ShapeDtypeStruct(x2d.shape, F32),
        compiler_params=pltpu.CompilerParams(dimension_semantics=("arbitrary",),
                                             vmem_limit_bytes=V7X_VMEM_LIMIT_BYTES),
        name="tail",
    )(uv, x2d, ga, mb, w_uv, w_out, g, w_up, w_down, g_final)


def _channel_dft_table():
    n = FOURIER_GROUP_DIM
    idx = jnp.arange(D_FOURIER, dtype=jnp.int32)
    same_group = (idx[:, None] // n) == (idx[None, :] // n)
    ang = ((idx[:, None] % n) * (idx[None, :] % n) % n).astype(F32) * (2.0 * math.pi / n)
    scale = 1.0 / math.sqrt(n)
    cos = jnp.where(same_group, jnp.cos(ang) * scale, 0.0)
    sin = jnp.where(same_group, jnp.sin(ang) * scale, 0.0)
    return jnp.concatenate([cos, -sin], axis=0).astype(BF16)


def _sequence_dft_tables(seq):
    idx = jnp.arange(seq // 2, dtype=jnp.int32)
    ang = ((idx[:, None] * idx[None, :]) % seq).astype(F32) * (2.0 * math.pi / seq)
    scale = 1.0 / math.sqrt(seq)
    return (jnp.cos(ang) * scale).astype(BF16), (jnp.sin(ang) * scale).astype(BF16)


def _reverse_table():
    idx = jnp.arange(REVERSE_ROWS, dtype=jnp.int32)
    return (idx[None, :] == REVERSE_ROWS - idx[:, None]).astype(BF16)


def kernel(x, g_mix, w_in, w_a, ln_v_g, ln_v_b, w_s, b_s, w_b, w_out, g_mlp, w_up, w_down, g_final):
    bsz, seq, d = x.shape
    depth = w_in.shape[0]
    assert d == D_MODEL and seq % TOKENS_PER_STEP == 0 and TOKENS_PER_STEP % ROWS_PER_SUBBLOCK == 0
    assert ROWS_PER_SUBBLOCK % CHUNK == 0 and (seq // 2) % REVERSE_ROWS == 0
    n_tok = bsz * seq

    cs = _channel_dft_table()
    cos_half, sin_half = _sequence_dft_tables(seq)
    rev = _reverse_table()
    row = lambda p: p.reshape(1, -1).astype(F32)
    g_final_row = row(g_final)

    x = x.reshape(n_tok, d)
    for l in range(depth):
        bias = jnp.repeat(b_s[l].T.astype(F32), SGU_HEAD_DIM, axis=1)
        a, ga, mb = _mixer_in(x, row(g_mix[l]), w_in[l].astype(BF16), row(ln_v_g[l]), row(ln_v_b[l]),
                              w_s[l].astype(BF16), bias, w_b[l].astype(BF16))
        uv = _fourier(a.reshape(bsz, seq, D_FOURIER), cos_half, sin_half, rev)
        x = _tail(uv.reshape(n_tok, d), x, ga, mb, _fold_w_a(cs, w_a[l].astype(BF16)), w_out[l].astype(BF16),
                  row(g_mlp[l]), w_up[l].astype(BF16), w_down[l].astype(BF16), g_final_row,
                  final_norm=(l == depth - 1))
    return x.reshape(bsz, seq, d)
```

```python
import functools
import math

import jax
import jax.numpy as jnp
from jax import lax
from jax.experimental import pallas as pl
from jax.experimental.pallas import tpu as pltpu

D_MODEL = 1024
N_FOURIER_GROUPS = 8
D_FOURIER = D_MODEL // 2
FOURIER_GROUP_DIM = D_FOURIER // N_FOURIER_GROUPS
CHUNK = 128
N_SGU_HEADS = 8
D_SGU = D_MODEL
SGU_HEAD_DIM = D_SGU // N_SGU_HEADS
D_FF = 4 * D_MODEL
EPS = 1e-6

C_A = D_FOURIER
C_U = C_A + D_SGU
C_V = C_U + D_SGU
C_GA = C_V + D_MODEL
IN_COLS = C_GA + D_MODEL

V7X_VMEM_LIMIT_BYTES = 56 * 1024 * 1024

TOKENS_PER_STEP = 1024
ROWS_PER_SUBBLOCK = 256
FF_COLS_PER_DOT = 1024
REVERSE_ROWS = 256

BF16 = jnp.bfloat16
F32 = jnp.float32


def _dot(a, b):
    return jnp.dot(a, b, preferred_element_type=F32)


def _rms_norm(x, g):
    return x * lax.rsqrt(jnp.mean(x * x, axis=-1, keepdims=True) + EPS) * g


def _gelu_tanh(x):
    c = math.sqrt(2.0 / math.pi)
    return 0.5 * x * (1.0 + jnp.tanh(c * (x + 0.044715 * (x * x * x))))


def _sigmoid(x):
    return 1.0 / (1.0 + jnp.exp(-x))


def _resident(shape):
    return pl.BlockSpec(shape, lambda *_: (0,) * len(shape), pipeline_mode=pl.Buffered(1))


def _resident_layer(layer, shape):
    return pl.BlockSpec((None,) + shape, lambda *_: (layer,) + (0,) * len(shape),
                        pipeline_mode=pl.Buffered(1))


def _mixer_in_rows(r0, nrows, x_ref, g_ref, w_in_ref, ln_g_ref, ln_b_ref, w_s_ref, bias_ref, w_b_ref,
                   a_ref, ga_ref, mb_ref):
    rows = slice(r0, r0 + nrows)
    hb = _rms_norm(x_ref[rows, :], g_ref[...]).astype(BF16)

    v = _gelu_tanh(_dot(hb, w_in_ref[:, C_U:C_V]))
    mu = jnp.mean(v, axis=-1, keepdims=True)
    vc = v - mu
    vn = vc * lax.rsqrt(jnp.mean(vc * vc, axis=-1, keepdims=True) + EPS)
    vn = (vn * ln_g_ref[...] + ln_b_ref[...]).astype(BF16)
    u = _gelu_tanh(_dot(hb, w_in_ref[:, C_A:C_U]))

    a_ref[rows, :] = _dot(hb, w_in_ref[:, 0:C_A]).astype(BF16)
    ga_ref[rows, :] = _sigmoid(_dot(hb, w_in_ref[:, C_V:C_GA])).astype(BF16)
    gb = _sigmoid(_dot(hb, w_in_ref[:, C_GA:IN_COLS]))

    mixed_rows = []
    for c in range(nrows // CHUNK):
        c_r0 = c * CHUNK
        cols = []
        for hd in range(N_SGU_HEADS):
            c0 = hd * SGU_HEAD_DIM
            cols.append(_dot(w_s_ref[hd], vn[c_r0:c_r0 + CHUNK, c0:c0 + SGU_HEAD_DIM]))
        mixed_rows.append(jnp.concatenate(cols, axis=1) + bias_ref[...])
    mixed = mixed_rows[0] if len(mixed_rows) == 1 else jnp.concatenate(mixed_rows, axis=0)
    s = (u * mixed).astype(BF16)
    mb_ref[rows, :] = (gb * _dot(s, w_b_ref[...])).astype(BF16)


def _mixer_in_body(*refs):
    for r0 in range(0, TOKENS_PER_STEP, ROWS_PER_SUBBLOCK):
        _mixer_in_rows(r0, ROWS_PER_SUBBLOCK, *refs)


def _mixer_in(layer, x2d, g, w_in, ln_g, ln_b, w_s, bias, w_b):
    n_tok = x2d.shape[0]
    t = TOKENS_PER_STEP
    tok = lambda width: pl.BlockSpec((t, width), lambda i: (i, 0))
    out = lambda width: jax.ShapeDtypeStruct((n_tok, width), BF16)
    return pl.pallas_call(
        _mixer_in_body,
        grid=(n_tok // t,),
        in_specs=[tok(D_MODEL), _resident((1, D_MODEL)), _resident_layer(layer, (D_MODEL, IN_COLS)),
                  _resident((1, D_SGU)), _resident((1, D_SGU)),
                  _resident_layer(layer, (N_SGU_HEADS, CHUNK, CHUNK)), _resident((CHUNK, D_SGU)),
                  _resident_layer(layer, (D_SGU, D_MODEL))],
        out_specs=[tok(D_FOURIER), tok(D_MODEL), tok(D_MODEL)],
        out_shape=[out(D_FOURIER), out(D_MODEL), out(D_MODEL)],
        compiler_params=pltpu.CompilerParams(dimension_semantics=("arbitrary",),
                                             vmem_limit_bytes=V7X_VMEM_LIMIT_BYTES),
        name="mixer_in",
    )(x2d, g, w_in, ln_g, ln_b, w_s, bias, w_b)


def _reverse_shifted(rev_ref, block, first_row):
    y = _dot(rev_ref[...], block)
    return jnp.where(lax.broadcasted_iota(jnp.int32, y.shape, 0) == 0, first_row, y)


def _fourier_body(a_ref, cos_ref, sin_ref, rev_ref, uv_ref):
    seq, width = a_ref.shape[1], a_ref.shape[2]
    half = seq // 2
    r = REVERSE_ROWS
    nb = half // r
    inv_sqrt_n = 1.0 / math.sqrt(seq)
    zero_row = jnp.zeros((1, width), F32)

    def bottom_block(q):
        return a_ref[0, half + q * r:half + (q + 1) * r, :]

    even, odd = [], []
    for p in range(nb):
        top = a_ref[0, p * r:(p + 1) * r, :].astype(F32)
        first = zero_row if p == 0 else bottom_block(nb - p)[0:1, :].astype(F32)
        mirrored = _reverse_shifted(rev_ref, bottom_block(nb - 1 - p), first)
        even.append((top + mirrored).astype(BF16))
        odd.append((top - mirrored).astype(BF16))
    a_even = jnp.concatenate(even, axis=0)
    a_odd = jnp.concatenate(odd, axis=0)

    nyquist = bottom_block(0)[0:1, :].astype(F32)
    sign = (1 - 2 * (lax.broadcasted_iota(jnp.int32, (half, 1), 0) & 1)).astype(F32)
    u = _dot(cos_ref[...], a_even) + sign * (nyquist * inv_sqrt_n)
    v = _dot(sin_ref[...], a_odd)
    uv_ref[0, 0:half, :] = jnp.concatenate([u, v], axis=1).astype(BF16)

    u_half = (jnp.sum(a_even.astype(F32) * sign, axis=0, keepdims=True) + nyquist) * inv_sqrt_n
    z = jnp.concatenate([u, -v], axis=1).astype(BF16)
    for p in range(nb):
        if p == 0:
            first = jnp.concatenate([u_half, zero_row], axis=1)
        else:
            first = z[(nb - p) * r:(nb - p) * r + 1, :].astype(F32)
        block = _reverse_shifted(rev_ref, z[(nb - 1 - p) * r:(nb - p) * r, :], first)
        uv_ref[0, half + p * r:half + (p + 1) * r, :] = block.astype(BF16)


def _fourier(a, cos_half, sin_half, rev):
    bsz, seq, width = a.shape
    half = seq // 2
    return pl.pallas_call(
        _fourier_body,
        grid=(bsz,),
        in_specs=[pl.BlockSpec((1, seq, width), lambda b: (b, 0, 0)),
                  _resident((half, half)), _resident((half, half)), _resident((REVERSE_ROWS, REVERSE_ROWS))],
        out_specs=pl.BlockSpec((1, seq, 2 * width), lambda b: (b, 0, 0)),
        out_shape=jax.ShapeDtypeStruct((bsz, seq, 2 * width), BF16),
        compiler_params=pltpu.CompilerParams(dimension_semantics=("arbitrary",),
                                             vmem_limit_bytes=V7X_VMEM_LIMIT_BYTES),
        name="fourier",
    )(a, cos_half, sin_half, rev)


def _fold_w_a_body(cs_ref, w_a_ref, o_ref):
    o_ref[...] = _dot(cs_ref[...], w_a_ref[...]).astype(BF16)


def _fold_w_a(layer, cs, w_a):
    return pl.pallas_call(
        _fold_w_a_body,
        grid=(1,),
        in_specs=[_resident(cs.shape), _resident_layer(layer, w_a.shape[1:])],
        out_specs=_resident((cs.shape[0], w_a.shape[2])),
        out_shape=jax.ShapeDtypeStruct((cs.shape[0], w_a.shape[2]), BF16),
        name="fold_w_a",
    )(cs, w_a)


def _tail_rows(r0, nrows, uv_ref, x_ref, ga_ref, mb_ref, w_uv_ref, w_out_ref, g_ref, w_up_ref, w_down_ref,
               g_final_ref, o_ref, *, final_norm):
    rows = slice(r0, r0 + nrows)
    ya = _dot(uv_ref[rows, :], w_uv_ref[...])
    merged = ga_ref[rows, :].astype(F32) * ya + mb_ref[rows, :].astype(F32)
    x = x_ref[rows, :] + _dot(merged.astype(BF16), w_out_ref[...])
    hb = _rms_norm(x, g_ref[...]).astype(BF16)
    acc = x
    for c0 in range(0, D_FF, FF_COLS_PER_DOT):
        f = jnp.maximum(_dot(hb, w_up_ref[:, c0:c0 + FF_COLS_PER_DOT]), 0.0)
        acc = acc + _dot((f * f).astype(BF16), w_down_ref[c0:c0 + FF_COLS_PER_DOT, :])
    if final_norm:
        acc = _rms_norm(acc, g_final_ref[...])
    o_ref[rows, :] = acc


def _tail_body(*refs, final_norm):
    for r0 in range(0, TOKENS_PER_STEP, ROWS_PER_SUBBLOCK):
        _tail_rows(r0, ROWS_PER_SUBBLOCK, *refs, final_norm=final_norm)


def _tail(layer, uv, x2d, ga, mb, w_uv, w_out, g, w_up, w_down, g_final, final_norm):
    n_tok = x2d.shape[0]
    t = TOKENS_PER_STEP
    tok = pl.BlockSpec((t, D_MODEL), lambda i: (i, 0))
    return pl.pallas_call(
        functools.partial(_tail_body, final_norm=final_norm),
        grid=(n_tok // t,),
        in_specs=[tok, tok, tok, tok,
                  _resident((2 * D_FOURIER, D_MODEL)), _resident_layer(layer, (D_MODEL, D_MODEL)),
                  _resident((1, D_MODEL)), _resident_layer(layer, (D_MODEL, D_FF)),
                  _resident_layer(layer, (D_FF, D_MODEL)), _resident((1, D_MODEL))],
        out_specs=tok,
        out_shape=jax.ShapeDtypeStruct(x2d.shape, F32),
        compiler_params=pltpu.CompilerParams(dimension_semantics=("arbitrary",),
                                             vmem_limit_bytes=V7X_VMEM_LIMIT_BYTES),
        name="tail",
    )(uv, x2d, ga, mb, w_uv, w_out, g, w_up, w_down, g_final)


def _channel_dft_table():
    n = FOURIER_GROUP_DIM
    idx = jnp.arange(D_FOURIER, dtype=jnp.int32)
    same_group = (idx[:, None] // n) == (idx[None, :] // n)
    ang = ((idx[:, None] % n) * (idx[None, :] % n) % n).astype(F32) * (2.0 * math.pi / n)
    scale = 1.0 / math.sqrt(n)
    cos = jnp.where(same_group, jnp.cos(ang) * scale, 0.0)
    sin = jnp.where(same_group, jnp.sin(ang) * scale, 0.0)
    return jnp.concatenate([cos, -sin], axis=0).astype(BF16)


def _sequence_dft_tables(seq):
    idx = jnp.arange(seq // 2, dtype=jnp.int32)
    ang = ((idx[:, None] * idx[None, :]) % seq).astype(F32) * (2.0 * math.pi / seq)
    scale = 1.0 / math.sqrt(seq)
    return (jnp.cos(ang) * scale).astype(BF16), (jnp.sin(ang) * scale).astype(BF16)


def _reverse_table():
    idx = jnp.arange(REVERSE_ROWS, dtype=jnp.int32)
    return (idx[None, :] == REVERSE_ROWS - idx[:, None]).astype(BF16)


def kernel(x, g_mix, w_in, w_a, ln_v_g, ln_v_b, w_s, b_s, w_b, w_out, g_mlp, w_up, w_down, g_final):
    bsz, seq, d = x.shape
    depth = w_in.shape[0]
    assert d == D_MODEL and seq % TOKENS_PER_STEP == 0 and TOKENS_PER_STEP % ROWS_PER_SUBBLOCK == 0
    assert ROWS_PER_SUBBLOCK % CHUNK == 0 and (seq // 2) % REVERSE_ROWS == 0
    n_tok = bsz * seq

    cs = _channel_dft_table()
    cos_half, sin_half = _sequence_dft_tables(seq)
    rev = _reverse_table()
    row = lambda p: p.reshape(1, -1).astype(F32)
    g_final_row = row(g_final)

    w_in, w_a, w_s, w_b, w_out, w_up, w_down = (w.astype(BF16) for w in (w_in, w_a, w_s, w_b, w_out, w_up, w_down))

    x = x.reshape(n_tok, d)
    for l in range(depth):
        bias = jnp.repeat(b_s[l].T.astype(F32), SGU_HEAD_DIM, axis=1)
        a, ga, mb = _mixer_in(l, x, row(g_mix[l]), w_in, row(ln_v_g[l]), row(ln_v_b[l]), w_s, bias, w_b)
        uv = _fourier(a.reshape(bsz, seq, D_FOURIER), cos_half, sin_half, rev)
        x = _tail(l, uv.reshape(n_tok, d), x, ga, mb, _fold_w_a(l, cs, w_a), w_out, row(g_mlp[l]), w_up, w_down,
                  g_final_row, final_norm=(l == depth - 1))
    return x.reshape(bsz, seq, d)
```
